```python
import math
import jax, jax.numpy as jnp
from jax import lax
import numpy as np

D_MODEL = 2048
BATCH = 4
SEQ = 2048
DEPTH = 2

EPS = 1e-6
DSW_HEADS = 8
DSW_HEAD_DIM = D_MODEL // 16
DSW_WIDTH = DSW_HEADS * DSW_HEAD_DIM
DSW_PATTERNS = ((128, 1), (512, 4), (2048, 16))
DSW_BLOCK = 128
GLA_HEADS = 4
GLA_DK = D_MODEL // 16
GLA_DV = D_MODEL // 8
GLA_QK_WIDTH = GLA_HEADS * GLA_DK
GLA_WIDTH = GLA_HEADS * GLA_DV
GLA_RANK = 16
GLA_TAU = 16.0
S5_WIDTH = D_MODEL // 2
S5_GROUP = 16
S5_GROUPS = S5_WIDTH // S5_GROUP
S5_STATE = 64
S5_DT_MIN = 1e-3
S5_DT_MAX = 1e-1
HGRN_HEADS = 8
HGRN_DIM = D_MODEL // 16
HGRN_WIDTH = HGRN_HEADS * HGRN_DIM
CHUNK = 64
AB_WIDTH = DSW_WIDTH + GLA_WIDTH
CD_WIDTH = S5_WIDTH + HGRN_WIDTH
AB_SIZES = (DSW_WIDTH, DSW_WIDTH, DSW_WIDTH, GLA_QK_WIDTH, GLA_QK_WIDTH, GLA_WIDTH, GLA_RANK, AB_WIDTH)
CD_SIZES = (S5_WIDTH, HGRN_WIDTH, HGRN_WIDTH, HGRN_WIDTH, CD_WIDTH)
AB_IN = sum(AB_SIZES)
CD_IN = sum(CD_SIZES)
AB_SPLIT = tuple(int(v) for v in np.cumsum(AB_SIZES)[:-1])
CD_SPLIT = tuple(int(v) for v in np.cumsum(CD_SIZES)[:-1])
N_EVEN = (DEPTH + 1) // 2
N_ODD = DEPTH // 2

kernel_name = 'hybrid_dilated_gla_s5_hgrn2_trunk'


def rmsnorm(x, g):
    xf = x.astype(jnp.float32)
    return xf * lax.rsqrt(jnp.mean(xf * xf, axis=-1, keepdims=True) + EPS) * g


def head_rmsnorm(o, g):
    o = o * lax.rsqrt(jnp.mean(o * o, axis=-1, keepdims=True) + EPS)
    return o.reshape(o.shape[0], o.shape[1], -1) * g


def dilated_pattern(q, k, v, window, dilation):
    B, S, H, E = q.shape
    d = dilation
    M = S // d
    wc = window // d
    blk = DSW_BLOCK
    nb = -(-M // blk)
    Mp = nb * blk

    def strided(t):
        return jnp.pad(t.reshape(B, M, d, H, E), ((0, 0), (0, Mp - M), (0, 0), (0, 0), (0, 0)))

    def band(t):
        tp = jnp.pad(t, ((0, 0), (blk, 0), (0, 0), (0, 0), (0, 0)))
        prev = tp[:, :Mp].reshape(B, nb, blk, d, H, E)
        cur = t.reshape(B, nb, blk, d, H, E)
        return jnp.concatenate([prev, cur], axis=2)

    qb = strided(q).reshape(B, nb, blk, d, H, E)
    kb = band(strided(k))
    vb = band(strided(v))
    s = jnp.einsum('bnidhe,bnjdhe->bndhij', qb, kb) * (E ** -0.5)
    i_idx = np.arange(blk)[:, None]
    j_idx = np.arange(2 * blk)[None, :]
    n_idx = np.arange(nb)[:, None, None]
    offset = blk + i_idx - j_idx
    key_pos = (n_idx - 1) * blk + j_idx
    mask = (offset >= 0) & (offset <= wc) & (key_pos >= 0)
    s = jnp.where(mask[None, :, None, None], s, -jnp.inf)
    m = jnp.max(s, axis=-1, keepdims=True)
    p = jnp.exp(s - m)
    den = jnp.sum(p, axis=-1)
    o = jnp.einsum('bndhij,bnjdhe->bnidhe', p, vb) / jnp.transpose(den, (0, 1, 4, 2, 3))[..., None]
    lse = jnp.transpose(m[..., 0] + jnp.log(den), (0, 1, 4, 2, 3))
    o = o.reshape(B, Mp, d, H, E)[:, :M].reshape(B, S, H, E)
    lse = lse.reshape(B, Mp, d, H)[:, :M].reshape(B, S, H)
    return o, lse


def dilated_attention(q, k, v):
    outs, lses = [], []
    for window, dilation in DSW_PATTERNS:
        o, lse = dilated_pattern(q, k, v, window, dilation)
        outs.append(o)
        lses.append(lse)
    w = jax.nn.softmax(jnp.stack(lses, axis=0), axis=0)
    return jnp.einsum('pbsh,pbshe->bshe', w, jnp.stack(outs, axis=0))


def chunked_gated_linear_attention(q, k, v, log_g):
    B, S, H, K = q.shape
    V = v.shape[-1]
    n = S // CHUNK

    def blocks(t):
        return jnp.moveaxis(t.reshape(B, n, CHUNK, H, t.shape[-1]), 1, 0)

    qc, kc, vc = blocks(q), blocks(k), blocks(v)
    bc = jnp.cumsum(blocks(log_g), axis=2)
    causal = np.tril(np.ones((CHUNK, CHUNK), dtype=bool))[None, :, :, None, None]

    def step(state, inp):
        qi, ki, vi, bi = inp
        inter = jnp.einsum('bihk,bhkv->bihv', qi * jnp.exp(bi), state)
        diff = bi[:, :, None] - bi[:, None, :]
        decay = jnp.exp(jnp.where(causal, diff, -jnp.inf))
        scores = jnp.einsum('bihk,bjhk,bijhk->bhij', qi, ki, decay)
        intra = jnp.einsum('bhij,bjhv->bihv', scores, vi)
        b_last = bi[:, -1]
        k_dec = ki * jnp.exp(b_last[:, None] - bi)
        state = state * jnp.exp(b_last)[..., None] + jnp.einsum('bjhk,bjhv->bhkv', k_dec, vi)
        return state, inter + intra

    state0 = jnp.zeros((B, H, K, V), jnp.float32)
    _, out = lax.scan(step, state0, (qc, kc, vc, bc))
    return jnp.moveaxis(out, 0, 1).reshape(B, S, H, V)


def _cmul(ar, ai, br, bi):
    return ar * br - ai * bi, ar * bi + ai * br


def _s5_combine(e1, e2):
    a1r, a1i, b1r, b1i = e1
    a2r, a2i, b2r, b2i = e2
    ar, ai = _cmul(a2r, a2i, a1r, a1i)
    br, bi = _cmul(a2r, a2i, b1r, b1i)
    return ar, ai, br + b2r, bi + b2i


def s5_mixer(u, lam_re, lam_im, log_dt, b_re, b_im, c_re, c_im, d_skip, w_glu, b_glu):
    B, S, _ = u.shape
    f32 = jnp.float32
    lam_re, lam_im = lam_re.astype(f32), lam_im.astype(f32)
    b_re, b_im, c_re, c_im = b_re.astype(f32), b_im.astype(f32), c_re.astype(f32), c_im.astype(f32)
    ug = u.reshape(B, S, S5_GROUPS, S5_GROUP)
    dt = jnp.exp(log_dt.astype(f32))[:, None]
    mag = jnp.exp(lam_re * dt)
    ang = lam_im * dt
    ab_re, ab_im = mag * jnp.cos(ang), mag * jnp.sin(ang)
    inv = 1.0 / (lam_re * lam_re + lam_im * lam_im)
    nr, ni = ab_re - 1.0, ab_im
    co_re = (nr * lam_re + ni * lam_im) * inv
    co_im = (ni * lam_re - nr * lam_im) * inv
    bb_re, bb_im = _cmul(co_re[..., None], co_im[..., None], b_re, b_im)
    ut = jnp.swapaxes(ug, 0, 1)
    bu_re = jnp.einsum('gph,sbgh->sbgp', bb_re, ut)
    bu_im = jnp.einsum('gph,sbgh->sbgp', bb_im, ut)
    a_re = jnp.broadcast_to(ab_re, (S, 1, S5_GROUPS, S5_STATE))
    a_im = jnp.broadcast_to(ab_im, (S, 1, S5_GROUPS, S5_STATE))
    _, _, x_re, x_im = lax.associative_scan(_s5_combine, (a_re, a_im, bu_re, bu_im), axis=0)
    y = (jnp.einsum('ghp,sbgp->bsgh', c_re, x_re) - jnp.einsum('ghp,sbgp->bsgh', c_im, x_im)
         + d_skip * ug)
    y = jax.nn.gelu(y.reshape(B, S, S5_WIDTH))
    return y * jax.nn.sigmoid(y @ w_glu + b_glu)


def ab_layer(h, w_in, gla_w_gate, gla_b_gate, gla_norm_g, w_out):
    B, S, _ = h.shape
    z = h @ w_in
    qa, ka, va, qb, kb, vb, g_low, gate = jnp.split(z, AB_SPLIT, axis=-1)
    sh_a = (B, S, DSW_HEADS, DSW_HEAD_DIM)
    ya = dilated_attention(qa.reshape(sh_a), ka.reshape(sh_a), va.reshape(sh_a)).reshape(B, S, DSW_WIDTH)
    log_a = jax.nn.log_sigmoid(g_low @ gla_w_gate + gla_b_gate) / GLA_TAU
    sh_k = (B, S, GLA_HEADS, GLA_DK)
    ob = chunked_gated_linear_attention((qb * GLA_DK ** -0.5).reshape(sh_k), kb.reshape(sh_k),
                                        vb.reshape(B, S, GLA_HEADS, GLA_DV), log_a.reshape(sh_k))
    yb = head_rmsnorm(ob, gla_norm_g)
    y = jnp.concatenate([ya, yb], axis=-1) * jax.nn.silu(gate)
    return y @ w_out


def cd_layer(h, w_in, lam_re, lam_im, log_dt, b_re, b_im, c_re, c_im, d_skip, w_glu, b_glu,
             lower_bound, hgrn_norm_g, w_out):
    B, S, _ = h.shape
    z = h @ w_in
    u, qd, fd, idd, gate = jnp.split(z, CD_SPLIT, axis=-1)
    yc = s5_mixer(u, lam_re, lam_im, log_dt, b_re, b_im, c_re, c_im, d_skip, w_glu, b_glu)
    f = lower_bound + (1.0 - lower_bound) * jax.nn.sigmoid(fd)
    sh = (B, S, HGRN_HEADS, HGRN_DIM)
    od = chunked_gated_linear_attention(jax.nn.silu(qd).reshape(sh), (1.0 - f).reshape(sh),
                                        idd.reshape(sh), jnp.log(f).reshape(sh))
    yd = head_rmsnorm(od, hgrn_norm_g)
    y = jnp.concatenate([yc, yd], axis=-1) * jax.nn.silu(gate)
    return y @ w_out


def setup_inputs(seed: int = 0) -> dict:
    key = jax.random.key(seed)
    ks = jax.random.split(key, 24)
    f32 = jnp.float32

    def nrm(k, shape, s):
        return s * jax.random.normal(k, shape, f32)

    G, P, Hc = S5_GROUPS, S5_STATE, S5_GROUP
    return {
        'x': nrm(ks[0], (BATCH, SEQ, D_MODEL), 1.0),
        'norm_g': 1.0 + nrm(ks[1], (DEPTH, D_MODEL), 0.02),
        'final_g': 1.0 + nrm(ks[2], (D_MODEL,), 0.02),
        'ab_w_in': nrm(ks[3], (N_EVEN, D_MODEL, AB_IN), D_MODEL ** -0.5),
        'gla_w_gate': nrm(ks[4], (N_EVEN, GLA_RANK, GLA_QK_WIDTH), GLA_RANK ** -0.5),
        'gla_b_gate': nrm(ks[5], (N_EVEN, GLA_QK_WIDTH), 0.1),
        'gla_norm_g': 1.0 + nrm(ks[6], (N_EVEN, GLA_WIDTH), 0.02),
        'ab_w_out': nrm(ks[7], (N_EVEN, AB_WIDTH, D_MODEL), AB_WIDTH ** -0.5),
        'cd_w_in': nrm(ks[8], (N_ODD, D_MODEL, CD_IN), D_MODEL ** -0.5),
        's5_lam_re': -0.5 + nrm(ks[9], (N_ODD, G, P), 0.01),
        's5_lam_im': math.pi * jnp.arange(P, dtype=f32) + nrm(ks[10], (N_ODD, G, P), 0.01),
        's5_log_dt': jax.random.uniform(ks[11], (N_ODD, G), f32, math.log(S5_DT_MIN), math.log(S5_DT_MAX)),
        's5_b_re': nrm(ks[12], (N_ODD, G, P, Hc), (2 * Hc) ** -0.5),
        's5_b_im': nrm(ks[13], (N_ODD, G, P, Hc), (2 * Hc) ** -0.5),
        's5_c_re': nrm(ks[14], (N_ODD, G, Hc, P), 0.5),
        's5_c_im': nrm(ks[15], (N_ODD, G, Hc, P), 0.5),
        's5_d': nrm(ks[16], (N_ODD, G, Hc), 1.0),
        's5_w_glu': nrm(ks[17], (N_ODD, S5_WIDTH, S5_WIDTH), S5_WIDTH ** -0.5),
        's5_b_glu': nrm(ks[18], (N_ODD, S5_WIDTH), 0.1),
        'hgrn_gamma': nrm(ks[19], (DEPTH, HGRN_WIDTH), 1.0),
        'hgrn_norm_g': 1.0 + nrm(ks[20], (N_ODD, HGRN_WIDTH), 0.02),
        'cd_w_out': nrm(ks[21], (N_ODD, CD_WIDTH, D_MODEL), CD_WIDTH ** -0.5),
    }


def reference(x, norm_g, final_g, ab_w_in, gla_w_gate, gla_b_gate, gla_norm_g, ab_w_out,
              cd_w_in, s5_lam_re, s5_lam_im, s5_log_dt, s5_b_re, s5_b_im, s5_c_re, s5_c_im,
              s5_d, s5_w_glu, s5_b_glu, hgrn_gamma, hgrn_norm_g, cd_w_out):
    h = x.astype(jnp.float32)
    sm = jax.nn.softmax(hgrn_gamma.astype(jnp.float32), axis=0)
    lower_bounds = jnp.cumsum(sm, axis=0) - sm[0]
    for l in range(DEPTH):
        hn = rmsnorm(h, norm_g[l])
        j = l // 2
        if l % 2 == 0:
            h = h + ab_layer(hn, ab_w_in[j], gla_w_gate[j], gla_b_gate[j], gla_norm_g[j], ab_w_out[j])
        else:
            h = h + cd_layer(hn, cd_w_in[j], s5_lam_re[j], s5_lam_im[j], s5_log_dt[j], s5_b_re[j],
                             s5_b_im[j], s5_c_re[j], s5_c_im[j], s5_d[j], s5_w_glu[j], s5_b_glu[j],
                             lower_bounds[l], hgrn_norm_g[j], cd_w_out[j])
    return rmsnorm(h, final_g).astype(x.dtype)
```

```python
import functools
import math

import jax
import jax.numpy as jnp
from jax import lax
from jax.experimental import pallas as pl
from jax.experimental.pallas import tpu as pltpu

F32 = jnp.float32
BF16 = jnp.bfloat16

EPS = 1e-6
LANES = 128
VMEM_LIMIT_BYTES = 56 * 1024 * 1024

DSW_HEADS = 8
DSW_E = 128
DSW_BLOCK = 128
DSW_DILATIONS = (1, 4, 16)
GLA_HEADS = 4
GLA_DK = 128
GLA_DV = 256
GLA_RANK = 16
GLA_TAU = 16.0
S5_GROUP = 16
S5_STATE = 64
HGRN_HEADS = 8
HGRN_DIM = 128
CHUNK = 64
SUB = 16
S5_CHUNK = 128
S5_MID = S5_CHUNK // 2
S5_GROUPS_PER_BLOCK = LANES // S5_GROUP
S5_HALF = S5_GROUPS_PER_BLOCK * S5_STATE
NEG_BIG = -1e30


def _cparams(*sem):
    return pltpu.CompilerParams(dimension_semantics=sem, vmem_limit_bytes=VMEM_LIMIT_BYTES)


def _silu(x):
    return x * jax.nn.sigmoid(x)


def _norm_proj_kernel(*refs, slab, has_extra):
    if has_extra:
        x_ref, g_ref, w_ref, wx_ref, o_ref, ox_ref, xn_ref = refs
    else:
        x_ref, g_ref, w_ref, o_ref, xn_ref = refs
    tm = x_ref.shape[0]

    @pl.when(pl.program_id(1) == 0)
    def _():
        def body(s, c):
            r = pl.ds(pl.multiple_of(s * slab, slab), slab)
            x = x_ref[r, :]
            ms = jnp.mean(x * x, axis=-1, keepdims=True)
            xn_ref[r, :] = (x * lax.rsqrt(ms + EPS) * g_ref[...]).astype(BF16)
            return c
        lax.fori_loop(0, tm // slab, body, 0)
        if has_extra:
            ox_ref[...] = jnp.dot(xn_ref[...], wx_ref[...], preferred_element_type=F32)

    o_ref[...] = jnp.dot(xn_ref[...], w_ref[...], preferred_element_type=F32)


def _norm_proj(x, g, w, w_extra=None, *, tm=512, tn=512):
    m, d = x.shape
    n = w.shape[1]
    assert m % tm == 0 and n % tn == 0
    has_extra = w_extra is not None
    in_specs = [pl.BlockSpec((tm, d), lambda i, j: (i, 0)),
                pl.BlockSpec((1, d), lambda i, j: (0, 0)),
                pl.BlockSpec((d, tn), lambda i, j: (0, j))]
    out_shape = [jax.ShapeDtypeStruct((m, n), F32)]
    out_specs = [pl.BlockSpec((tm, tn), lambda i, j: (i, j))]
    args = [x, g.reshape(1, d), w]
    if has_extra:
        nx = w_extra.shape[1]
        in_specs.append(pl.BlockSpec((d, nx), lambda i, j: (0, 0)))
        out_shape.append(jax.ShapeDtypeStruct((m, nx), F32))
        out_specs.append(pl.BlockSpec((tm, nx), lambda i, j: (i, 0)))
        args.append(w_extra)
    res = pl.pallas_call(
        functools.partial(_norm_proj_kernel, slab=64, has_extra=has_extra),
        grid=(m // tm, n // tn),
        in_specs=in_specs, out_specs=out_specs, out_shape=out_shape,
        scratch_shapes=[pltpu.VMEM((tm, d), BF16)],
        compiler_params=_cparams("arbitrary", "arbitrary"),
        name="norm_proj_extra" if has_extra else "norm_proj",
    )(*args)
    return res if has_extra else res[0]


def _dilated_attn_kernel(q_ref, k_ref, v_ref, gate_ref, o_ref, acc_ref, m_ref, l_ref):
    seq, e = q_ref.shape
    blk = DSW_BLOCK
    scale = e ** -0.5
    ii = lax.broadcasted_iota(jnp.int32, (blk, 2 * blk), 0)
    jj = lax.broadcasted_iota(jnp.int32, (blk, 2 * blk), 1)
    mask_band = jnp.where(jj < blk, jj - ii, ii - (jj - blk)) >= 0
    mask_cur = (lax.broadcasted_iota(jnp.int32, (blk, blk), 1)
                <= lax.broadcasted_iota(jnp.int32, (blk, blk), 0))

    def attend(q, k, v, mask):
        s = lax.dot_general(q.astype(BF16), k.astype(BF16), (((1,), (1,)), ((), ())),
                            preferred_element_type=F32) * scale
        s = jnp.where(mask, s, NEG_BIG)
        m = jnp.max(s, axis=-1, keepdims=True)
        p = jnp.exp(s - m)
        l = jnp.sum(p, axis=-1, keepdims=True)
        acc = jnp.dot(p.astype(BF16), v.astype(BF16), preferred_element_type=F32)
        return acc, m, l

    def commit(rows, acc, m, l, first, last):
        m = jnp.broadcast_to(m, (blk, e))
        l = jnp.broadcast_to(l, (blk, e))
        if not first:
            m_old = m_ref[rows, :]
            m_new = jnp.maximum(m_old, m)
            a = jnp.exp(m_old - m_new)
            b = jnp.exp(m - m_new)
            acc = a * acc_ref[rows, :] + b * acc
            l = a * l_ref[rows, :] + b * l
            m = m_new
        if last:
            acc_ref[rows, :] = acc / l
        else:
            acc_ref[rows, :] = acc
            m_ref[rows, :] = m
            l_ref[rows, :] = l

    def run_pattern(d, first, last):
        nb = seq // (d * blk)

        def rows_of(r, n, count):
            if d == 1:
                return pl.ds(pl.multiple_of(n * blk, blk), count)
            return pl.ds(r + n * (d * blk), count, stride=d)

        def block0(r):
            rows = pl.ds(0, blk) if d == 1 else rows_of(r, 0, blk)
            acc, m, l = attend(q_ref[rows, :], k_ref[rows, :], v_ref[rows, :], mask_cur)
            commit(rows, acc, m, l, first, last)

        def blockn(r, n):
            rows = rows_of(r, n, blk)
            band = rows_of(r, n - 1, 2 * blk)
            acc, m, l = attend(q_ref[rows, :], k_ref[band, :], v_ref[band, :], mask_band)
            commit(rows, acc, m, l, first, last)

        def per_residue(r, c):
            block0(r)
            if nb > 1:
                lax.fori_loop(1, nb, lambda n, cc: (blockn(r, n), cc)[1], 0)
            return c

        if d == 1:
            per_residue(0, 0)
        else:
            lax.fori_loop(0, d, per_residue, 0)

    for idx, d in enumerate(DSW_DILATIONS):
        run_pattern(d, idx == 0, idx == len(DSW_DILATIONS) - 1)

    def finish(s, c):
        rows = pl.ds(pl.multiple_of(s * blk, blk), blk)
        o_ref[rows, :] = (acc_ref[rows, :] * _silu(gate_ref[rows, :])).astype(o_ref.dtype)
        return c
    lax.fori_loop(0, seq // blk, finish, 0)


def _dilated_attention(z3, *, q_col, k_col, v_col, gate_col):
    b, s, _ = z3.shape
    e = DSW_E

    def spec(col0):
        return pl.BlockSpec((None, s, e), lambda bi, h, col0=col0: (bi, 0, col0 + h))

    return pl.pallas_call(
        _dilated_attn_kernel,
        grid=(b, DSW_HEADS),
        in_specs=[spec(q_col), spec(k_col), spec(v_col), spec(gate_col)],
        out_specs=pl.BlockSpec((None, s, e), lambda bi, h: (bi, 0, h)),
        out_shape=jax.ShapeDtypeStruct((b, s, DSW_HEADS * e), BF16),
        scratch_shapes=[pltpu.VMEM((s, e), F32)] * 3,
        compiler_params=_cparams("arbitrary", "arbitrary"),
        name="dilated_attention",
    )(z3, z3, z3, z3)


def _chunk_cumsum(g, chunk):
    pos = lax.broadcasted_iota(jnp.int32, g.shape, 0) & (chunk - 1)
    shift = 1
    while shift < chunk:
        g = g + jnp.where(pos >= shift, pltpu.roll(g, shift, 0), 0.0)
        shift *= 2
    return g


def _gated_chunks(q_ref, k_ref, v_ref, b_ref, st_ref, emit):
    t, kdim = q_ref.shape
    nsub = CHUNK // SUB
    row = lax.broadcasted_iota(jnp.int32, (CHUNK, kdim), 0)
    ii = lax.broadcasted_iota(jnp.int32, (CHUNK, CHUNK), 0)
    jj = lax.broadcasted_iota(jnp.int32, (CHUNK, CHUNK), 1)
    causal = jj <= ii

    def body(c, carry):
        rows = pl.ds(pl.multiple_of(c * CHUNK, CHUNK), CHUNK)
        q = q_ref[rows, :]
        k = k_ref[rows, :]
        v = v_ref[rows, :].astype(BF16)
        b = b_ref[rows, :]
        b_last = b[CHUNK - 1:CHUNK, :]
        state = st_ref[...]
        inter = lax.dot_general((q * jnp.exp(b)).astype(BF16), state.astype(BF16),
                                (((1,), (1,)), ((), ())), preferred_element_type=F32)
        q_parts, k_parts = [], []
        for i in range(nsub):
            ref_row = b[i * SUB + SUB // 2:i * SUB + SUB // 2 + 1, :]
            in_block = (row >= i * SUB) & (row < (i + 1) * SUB)
            q_parts.append(jnp.where(in_block, q * jnp.exp(b - ref_row), 0.0).astype(BF16))
            k_parts.append(jnp.where(row < (i + 1) * SUB, k * jnp.exp(ref_row - b), 0.0).astype(BF16))
        scores = lax.dot_general(jnp.concatenate(q_parts, axis=1), jnp.concatenate(k_parts, axis=1),
                                 (((1,), (1,)), ((), ())), preferred_element_type=F32)
        scores = jnp.where(causal, scores, 0.0)
        o = inter + jnp.dot(scores.astype(BF16), v, preferred_element_type=F32)
        k_dec = (k * jnp.exp(b_last - b)).astype(BF16)
        st_ref[...] = state * jnp.exp(b_last) + lax.dot_general(
            v, k_dec, (((0,), (0,)), ((), ())), preferred_element_type=F32)
        emit(rows, o)
        return carry

    lax.fori_loop(0, t // CHUNK, body, 0)


def _head_norm_gate(o, ng, gate):
    o = o * lax.rsqrt(jnp.mean(o * o, axis=-1, keepdims=True) + EPS)
    return o * ng * _silu(gate)


def _log_sigmoid(x):
    return jnp.minimum(x, 0.0) - jnp.log(1.0 + jnp.exp(-jnp.abs(x)))


def _gla_kernel(q_ref, k_ref, v_ref, glow_ref, wg_ref, bg_ref, ng_ref, gate_ref, o_ref,
                qs_ref, b_ref, st_ref):
    @pl.when(pl.program_id(2) == 0)
    def _():
        st_ref[...] = jnp.zeros_like(st_ref)

    x = jnp.dot(glow_ref[...].astype(BF16), wg_ref[...], preferred_element_type=F32) + bg_ref[...]
    b_ref[...] = _chunk_cumsum(_log_sigmoid(x) / GLA_TAU, CHUNK)
    qs_ref[...] = q_ref[...] * (GLA_DK ** -0.5)

    def emit(rows, o):
        o_ref[rows, :] = _head_norm_gate(o, ng_ref[...], gate_ref[rows, :]).astype(o_ref.dtype)

    _gated_chunks(qs_ref, k_ref, v_ref, b_ref, st_ref, emit)


def _gla(z3, glow3, wg, bg, ng, *, q_col, k_col, v_col, gate_col, tblk=512):
    b, s, _ = z3.shape
    kd, vd = GLA_DK, GLA_DV
    assert s % tblk == 0

    def spec(width, col0):
        return pl.BlockSpec((None, tblk, width), lambda bi, h, t, col0=col0: (bi, t, col0 + h))

    return pl.pallas_call(
        _gla_kernel,
        grid=(b, GLA_HEADS, s // tblk),
        in_specs=[spec(kd, q_col), spec(kd, k_col), spec(vd, v_col),
                  pl.BlockSpec((None, tblk, LANES), lambda bi, h, t: (bi, t, 0)),
                  pl.BlockSpec((LANES, kd), lambda bi, h, t: (0, h)),
                  pl.BlockSpec((1, kd), lambda bi, h, t: (0, h)),
                  pl.BlockSpec((1, vd), lambda bi, h, t: (0, h)),
                  spec(vd, gate_col)],
        out_specs=pl.BlockSpec((None, tblk, vd), lambda bi, h, t: (bi, t, h)),
        out_shape=jax.ShapeDtypeStruct((b, s, GLA_HEADS * vd), BF16),
        scratch_shapes=[pltpu.VMEM((tblk, kd), F32), pltpu.VMEM((tblk, kd), F32),
                        pltpu.VMEM((vd, kd), F32)],
        compiler_params=_cparams("arbitrary", "arbitrary", "arbitrary"),
        name="gla",
    )(z3, z3, z3, glow3, wg, bg, ng, z3)


def _hgrn_kernel(q_ref, f_ref, v_ref, lb_ref, ng_ref, gate_ref, o_ref, qs_ref, ks_ref, b_ref, st_ref):
    @pl.when(pl.program_id(2) == 0)
    def _():
        st_ref[...] = jnp.zeros_like(st_ref)

    lb = lb_ref[...]
    f = lb + (1.0 - lb) * jax.nn.sigmoid(f_ref[...])
    ks_ref[...] = 1.0 - f
    b_ref[...] = _chunk_cumsum(jnp.log(f), CHUNK)
    qs_ref[...] = _silu(q_ref[...])

    def emit(rows, o):
        o_ref[rows, :] = _head_norm_gate(o, ng_ref[...], gate_ref[rows, :]).astype(o_ref.dtype)

    _gated_chunks(qs_ref, ks_ref, v_ref, b_ref, st_ref, emit)


def _hgrn(z3, lb, ng, *, q_col, f_col, v_col, gate_col, tblk=512):
    b, s, _ = z3.shape
    hd = HGRN_DIM
    assert s % tblk == 0

    def spec(col0):
        return pl.BlockSpec((None, tblk, hd), lambda bi, h, t, col0=col0: (bi, t, col0 + h))

    vec = pl.BlockSpec((1, hd), lambda bi, h, t: (0, h))
    return pl.pallas_call(
        _hgrn_kernel,
        grid=(b, HGRN_HEADS, s // tblk),
        in_specs=[spec(q_col), spec(f_col), spec(v_col), vec, vec, spec(gate_col)],
        out_specs=pl.BlockSpec((None, tblk, hd), lambda bi, h, t: (bi, t, h)),
        out_shape=jax.ShapeDtypeStruct((b, s, HGRN_HEADS * hd), BF16),
        scratch_shapes=[pltpu.VMEM((tblk, hd), F32)] * 3 + [pltpu.VMEM((hd, hd), F32)],
        compiler_params=_cparams("arbitrary", "arbitrary", "arbitrary"),
        name="hgrn2",
    )(z3, z3, z3, lb, ng, z3)


def _s5_kernel(u_ref, bw_ref, cw_ref, e_ref, f_ref, mc_ref, d_ref, tril_ref, y_ref, carry_ref):
    hw = S5_HALF

    @pl.when(pl.program_id(2) == 0)
    def _():
        carry_ref[...] = jnp.zeros_like(carry_ref)

    u = u_ref[...]
    z = jnp.dot(u.astype(BF16), bw_ref[...], preferred_element_type=F32)
    z_re, z_im = z[:, :hw], z[:, hw:]
    e_re, e_im = e_ref[:, :hw], e_ref[:, hw:]
    zt = jnp.concatenate([z_re * e_re - z_im * e_im, z_re * e_im + z_im * e_re], axis=1)
    w = jnp.dot(tril_ref[...], zt.astype(BF16), preferred_element_type=F32)
    c_re, c_im = carry_ref[:, :hw], carry_ref[:, hw:]
    m_re, m_im = mc_ref[:, :hw], mc_ref[:, hw:]
    w_re = w[:, :hw] + (c_re * m_re - c_im * m_im)
    w_im = w[:, hw:] + (c_re * m_im + c_im * m_re)
    f_re, f_im = f_ref[:, :hw], f_ref[:, hw:]
    x = jnp.concatenate([w_re * f_re - w_im * f_im, w_re * f_im + w_im * f_re], axis=1)
    carry_ref[...] = x[S5_CHUNK - 1:S5_CHUNK, :]
    y = jnp.dot(x.astype(BF16), cw_ref[...], preferred_element_type=F32)
    y_ref[...] = y + d_ref[...] * u


def _s5_scan(z3, bw, cw, e_tab, f_tab, mc, dskip, tril):
    b, s, _ = z3.shape
    nblk = bw.shape[0]
    lc = S5_CHUNK
    assert s % lc == 0
    return pl.pallas_call(
        _s5_kernel,
        grid=(b, nblk, s // lc),
        in_specs=[pl.BlockSpec((None, lc, LANES), lambda bi, j, t: (bi, t, j)),
                  pl.BlockSpec((None, LANES, 2 * S5_HALF), lambda bi, j, t: (j, 0, 0)),
                  pl.BlockSpec((None, 2 * S5_HALF, LANES), lambda bi, j, t: (j, 0, 0)),
                  pl.BlockSpec((None, lc, 2 * S5_HALF), lambda bi, j, t: (j, 0, 0)),
                  pl.BlockSpec((None, lc, 2 * S5_HALF), lambda bi, j, t: (j, 0, 0)),
                  pl.BlockSpec((None, 1, 2 * S5_HALF), lambda bi, j, t: (j, 0, 0)),
                  pl.BlockSpec((1, LANES), lambda bi, j, t: (0, j)),
                  pl.BlockSpec((lc, lc), lambda bi, j, t: (0, 0))],
        out_specs=pl.BlockSpec((None, lc, LANES), lambda bi, j, t: (bi, t, j)),
        out_shape=jax.ShapeDtypeStruct((b, s, nblk * LANES), F32),
        scratch_shapes=[pltpu.VMEM((1, 2 * S5_HALF), F32)],
        compiler_params=_cparams("arbitrary", "arbitrary", "arbitrary"),
        name="s5_scan",
    )(z3, bw, cw, e_tab, f_tab, mc, dskip, tril)


def _s5_tables(lam_re, lam_im, log_dt, b_re, b_im, c_re, c_im):
    g, p = lam_re.shape
    hc = S5_GROUP
    gb = S5_GROUPS_PER_BLOCK
    nblk = g // gb
    dt = jnp.exp(log_dt.astype(F32))[:, None]
    lr, ang = lam_re * dt, lam_im * dt
    mag = jnp.exp(lr)
    ab_re, ab_im = mag * jnp.cos(ang), mag * jnp.sin(ang)
    inv = 1.0 / (lam_re * lam_re + lam_im * lam_im)
    nr, ni = ab_re - 1.0, ab_im
    co_re = (nr * lam_re + ni * lam_im) * inv
    co_im = (ni * lam_re - nr * lam_im) * inv
    bb_re = co_re[..., None] * b_re - co_im[..., None] * b_im
    bb_im = co_re[..., None] * b_im + co_im[..., None] * b_re
    eye = jnp.eye(gb, dtype=F32)

    def blockdiag_in(bb):
        bb = bb.reshape(nblk, gb, p, hc)
        return jnp.einsum('jgph,gk->jghkp', bb, eye).reshape(nblk, gb * hc, gb * p)

    def blockdiag_out(cc):
        cc = cc.reshape(nblk, gb, hc, p)
        return jnp.einsum('jghp,gk->jgpkh', cc, eye).reshape(nblk, gb * p, gb * hc)

    bw = jnp.concatenate([blockdiag_in(bb_re), blockdiag_in(bb_im)], axis=2).astype(BF16)
    cw = jnp.concatenate([blockdiag_out(c_re), blockdiag_out(-c_im)], axis=1).astype(BF16)

    def powers(n):
        n = n.astype(F32)[:, None, None]
        m = jnp.exp(lr[None] * n)
        re = (m * jnp.cos(ang[None] * n)).reshape(-1, nblk, gb * p)
        im = (m * jnp.sin(ang[None] * n)).reshape(-1, nblk, gb * p)
        return jnp.transpose(jnp.concatenate([re, im], axis=2), (1, 0, 2))

    steps = jnp.arange(S5_CHUNK)
    e_tab = powers(S5_MID - steps)
    f_tab = powers(steps - S5_MID)
    mc = powers(jnp.array([S5_MID + 1]))
    return bw, cw, e_tab, f_tab, mc


def _glu_kernel(y_ref, gate_ref, w_ref, b_ref, o_ref):
    y = jax.nn.gelu(y_ref[...])
    s = jnp.dot(y.astype(BF16), w_ref[...], preferred_element_type=F32) + b_ref[...]
    o_ref[...] = (y * jax.nn.sigmoid(s) * _silu(gate_ref[...])).astype(o_ref.dtype)


def _s5_glu(y2, z2, w_glu, b_glu, *, gate_col, tm=256):
    m, w = y2.shape
    return pl.pallas_call(
        _glu_kernel,
        grid=(m // tm,),
        in_specs=[pl.BlockSpec((tm, w), lambda i: (i, 0)),
                  pl.BlockSpec((tm, w), lambda i: (i, gate_col)),
                  pl.BlockSpec((w, w), lambda i: (0, 0)),
                  pl.BlockSpec((1, w), lambda i: (0, 0))],
        out_specs=pl.BlockSpec((tm, w), lambda i: (i, 0)),
        out_shape=jax.ShapeDtypeStruct((m, w), BF16),
        compiler_params=_cparams("arbitrary"),
        name="s5_glu",
    )(y2, z2, w_glu, b_glu)


def _out_proj_kernel(a1_ref, a2_ref, w1_ref, w2_ref, h_ref, *rest, final):
    if final:
        g_ref, o_ref = rest
    else:
        (o_ref,) = rest
    acc = jnp.dot(a1_ref[...], w1_ref[...], preferred_element_type=F32)
    acc = acc + jnp.dot(a2_ref[...], w2_ref[...], preferred_element_type=F32)
    h = h_ref[...] + acc
    if final:
        h = h * lax.rsqrt(jnp.mean(h * h, axis=-1, keepdims=True) + EPS) * g_ref[...]
    o_ref[...] = h


def _out_proj(a1, a2, w, h, final_g=None, *, tm=256):
    m, k1 = a1.shape
    k2 = a2.shape[1]
    d = w.shape[1]
    final = final_g is not None
    in_specs = [pl.BlockSpec((tm, k1), lambda i: (i, 0)),
                pl.BlockSpec((tm, k2), lambda i: (i, 0)),
                pl.BlockSpec((k1, d), lambda i: (0, 0)),
                pl.BlockSpec((k2, d), lambda i: (0, 0)),
                pl.BlockSpec((tm, d), lambda i: (i, 0))]
    args = [a1, a2, w[:k1], w[k1:], h]
    if final:
        in_specs.append(pl.BlockSpec((1, d), lambda i: (0, 0)))
        args.append(final_g.reshape(1, d))
    return pl.pallas_call(
        functools.partial(_out_proj_kernel, final=final),
        grid=(m // tm,),
        in_specs=in_specs,
        out_specs=pl.BlockSpec((tm, d), lambda i: (i, 0)),
        out_shape=jax.ShapeDtypeStruct((m, d), F32),
        compiler_params=_cparams("arbitrary"),
        name="out_proj_final" if final else "out_proj",
    )(*args)


def kernel(x, norm_g, final_g, ab_w_in, gla_w_gate, gla_b_gate, gla_norm_g, ab_w_out, cd_w_in,
           s5_lam_re, s5_lam_im, s5_log_dt, s5_b_re, s5_b_im, s5_c_re, s5_c_im, s5_d, s5_w_glu,
           s5_b_glu, hgrn_gamma, hgrn_norm_g, cd_w_out):
    bsz, seq, d = x.shape
    m = bsz * seq
    h = x.astype(F32).reshape(m, d)

    dsw = DSW_HEADS * DSW_E
    gqk = GLA_HEADS * GLA_DK
    gv = GLA_HEADS * GLA_DV
    glow0 = 3 * dsw + 2 * gqk + gv
    w0 = ab_w_in[0]
    w_main = jnp.concatenate([w0[:, :glow0], w0[:, glow0 + GLA_RANK:]], axis=1).astype(BF16)
    w_glow = jnp.pad(w0[:, glow0:glow0 + GLA_RANK], ((0, 0), (0, LANES - GLA_RANK))).astype(BF16)
    z0, glow = _norm_proj(h, norm_g[0], w_main, w_glow)
    n0 = w_main.shape[1]
    z0 = z0.reshape(bsz, seq, n0)
    glow = glow.reshape(bsz, seq, LANES)
    gate0 = glow0
    ya = _dilated_attention(z0, q_col=0, k_col=dsw // LANES, v_col=2 * dsw // LANES,
                            gate_col=gate0 // LANES)
    wg = jnp.pad(gla_w_gate[0], ((0, LANES - GLA_RANK), (0, 0))).astype(BF16)
    yb = _gla(z0, glow, wg, gla_b_gate[0].reshape(1, gqk), gla_norm_g[0].reshape(1, gv),
              q_col=3 * dsw // LANES, k_col=(3 * dsw + gqk) // LANES,
              v_col=(3 * dsw + 2 * gqk) // GLA_DV, gate_col=(gate0 + dsw) // GLA_DV)
    h = _out_proj(ya.reshape(m, dsw), yb.reshape(m, gv), ab_w_out[0].astype(BF16), h)

    s5w = s5_lam_re.shape[1] * S5_GROUP
    hw = HGRN_HEADS * HGRN_DIM
    z1 = _norm_proj(h, norm_g[1], cd_w_in[0].astype(BF16))
    n1 = z1.shape[1]
    z1_3 = z1.reshape(bsz, seq, n1)
    gate1 = s5w + 3 * hw
    bw, cw, e_tab, f_tab, mc = _s5_tables(s5_lam_re[0], s5_lam_im[0], s5_log_dt[0], s5_b_re[0],
                                          s5_b_im[0], s5_c_re[0], s5_c_im[0])
    tril = jnp.tril(jnp.ones((S5_CHUNK, S5_CHUNK), F32)).astype(BF16)
    y5 = _s5_scan(z1_3, bw, cw, e_tab, f_tab, mc, s5_d[0].reshape(1, s5w), tril)
    yc = _s5_glu(y5.reshape(m, s5w), z1, s5_w_glu[0].astype(BF16), s5_b_glu[0].reshape(1, s5w),
                 gate_col=gate1 // s5w)
    sm = jax.nn.softmax(hgrn_gamma.astype(F32), axis=0)
    lower = (jnp.cumsum(sm, axis=0) - sm[0])[1].reshape(1, hw)
    yd = _hgrn(z1_3, lower, hgrn_norm_g[0].reshape(1, hw), q_col=s5w // HGRN_DIM,
               f_col=(s5w + hw) // HGRN_DIM, v_col=(s5w + 2 * hw) // HGRN_DIM,
               gate_col=(gate1 + s5w) // HGRN_DIM)
    out = _out_proj(yc, yd.reshape(m, hw), cd_w_out[0].astype(BF16), h, final_g)
    return out.reshape(bsz, seq, d).astype(x.dtype)
```
